```python
import math
import jax, jax.numpy as jnp
from jax import lax
import numpy as np

D_MODEL = 1024
BATCH = 8
SEQ = 4096
DEPTH = 4

GRID_W = 64
HEAD_DIM = 64
NA_HEADS = 4
NA_KH_MAX = 8
NA_KW = 16
WIN_HEADS = 4
WIN_KV_HEADS = 2
WINDOW = 128
WIN_BLOCK = 128
MLA_HEADS = 4
MLA_Q_RANK = 192
MLA_KV_RANK = 256
MLA_NOPE = 64
MLA_ROPE = 32
MLA_V = 64
AX_HEADS = 4
AX_KV_HEADS = 2
Q_BLOCK = 128
ROPE_THETA = 10000.0
N_BRANCH = 4
BRANCH_W = 256
N_GROUPS = 4
EXPERTS_PER_GROUP = 8
N_EXPERTS = N_GROUPS * EXPERTS_PER_GROUP
TOP_K = 2
D_EXPERT = 256
N_MOD = 6
EPS = 1e-6
NEG_INF = -1e30

IN_SIZES = (
    NA_HEADS * HEAD_DIM, NA_HEADS * HEAD_DIM, NA_HEADS * HEAD_DIM,
    WIN_HEADS * HEAD_DIM, WIN_KV_HEADS * HEAD_DIM, WIN_KV_HEADS * HEAD_DIM,
    MLA_Q_RANK, MLA_KV_RANK, MLA_ROPE,
    AX_HEADS * HEAD_DIM, AX_KV_HEADS * HEAD_DIM, AX_KV_HEADS * HEAD_DIM,
    N_BRANCH * D_MODEL,
)
D_IN = sum(IN_SIZES)

kernel_name = 'hybrid_gated_multimixer_hmoe_encoder'


def rmsnorm(x, g):
    xf = x.astype(jnp.float32)
    y = xf * lax.rsqrt(jnp.mean(xf * xf, axis=-1, keepdims=True) + EPS)
    return (y * g.astype(jnp.float32)).astype(x.dtype)


def to_heads(t, n):
    B, S, _ = t.shape
    return t.reshape(B, S, n, -1).transpose(0, 2, 1, 3)


def from_heads(t):
    B, H, S, d = t.shape
    return t.transpose(0, 2, 1, 3).reshape(B, S, H * d)


def rope_angles(pos, dim):
    inv = ROPE_THETA ** (-jnp.arange(0, dim, 2, dtype=jnp.float32) / dim)
    ang = pos.astype(jnp.float32)[:, None] * inv[None, :]
    return jnp.cos(ang), jnp.sin(ang)


def apply_rope(x, cos, sin):
    x1, x2 = jnp.split(x, 2, axis=-1)
    c = cos.astype(x.dtype)
    s = sin.astype(x.dtype)
    return jnp.concatenate([x1 * c - x2 * s, x1 * s + x2 * c], axis=-1)


def axial_rope(x, cos_r, sin_r, cos_c, sin_c):
    xr, xc = jnp.split(x, 2, axis=-1)
    return jnp.concatenate([apply_rope(xr, cos_r, sin_r), apply_rope(xc, cos_c, sin_c)], axis=-1)


def neighbourhood_attention(q, k, v, rel_bias, scale):
    B, H, S, dh = q.shape
    rows = S // GRID_W
    kh = min(NA_KH_MAX, rows)
    qg = q.reshape(B, H, rows, GRID_W, dh)
    kg = k.reshape(B, H, rows, GRID_W, dh)
    vg = v.reshape(B, H, rows, GRID_W, dh)
    r = jnp.arange(rows)
    row_start = jnp.clip(r - kh // 2, 0, rows - kh)
    key_rows = row_start[:, None] + jnp.arange(kh)[None, :]
    k_blk = kg[:, :, key_rows]
    v_blk = vg[:, :, key_rows]
    col = jnp.arange(GRID_W)
    col_start = jnp.clip(col - NA_KW // 2, 0, GRID_W - NA_KW)
    col_off = col[None, :] - col_start[:, None]
    col_in = (col_off >= 0) & (col_off < NA_KW)
    dr = key_rows - r[:, None]
    dc_idx = jnp.clip(col[None, :] - col[:, None], -(NA_KW - 1), NA_KW - 1) + NA_KW - 1
    bias = rel_bias.astype(jnp.float32)[:, dr + NA_KH_MAX - 1]
    bias = bias[..., dc_idx].transpose(0, 1, 3, 2, 4)
    s = jnp.einsum('bhrqd,bhrikd->bhrqik', qg, k_blk, preferred_element_type=jnp.float32) * scale
    s = jnp.where(col_in[:, None, :], s + bias[None], NEG_INF)
    p = jax.nn.softmax(s.reshape(B, H, rows, GRID_W, kh * GRID_W), axis=-1)
    p = p.reshape(B, H, rows, GRID_W, kh, GRID_W).astype(v.dtype)
    out = jnp.einsum('bhrqik,bhrikd->bhrqd', p, v_blk)
    return out.reshape(B, H, S, dh)


def window_attention(q, k, v, sink, scale):
    B, Hq, S, dh = q.shape
    Hkv = k.shape[1]
    G = Hq // Hkv
    nb = S // WIN_BLOCK
    qb = q.reshape(B, Hkv, G, nb, WIN_BLOCK, dh)
    pad = ((0, 0), (0, 0), (WIN_BLOCK, WIN_BLOCK), (0, 0))
    kp = jnp.pad(k, pad).reshape(B, Hkv, nb + 2, WIN_BLOCK, dh)
    vp = jnp.pad(v, pad).reshape(B, Hkv, nb + 2, WIN_BLOCK, dh)
    kw = jnp.concatenate([kp[:, :, :-2], kp[:, :, 1:-1], kp[:, :, 2:]], axis=3)
    vw = jnp.concatenate([vp[:, :, :-2], vp[:, :, 1:-1], vp[:, :, 2:]], axis=3)
    i = jnp.arange(WIN_BLOCK)
    m = jnp.arange(3 * WIN_BLOCK)
    rel = m[None, :] - WIN_BLOCK - i[:, None]
    s_abs = (jnp.arange(nb)[:, None] - 1) * WIN_BLOCK + m[None, :]
    valid = (jnp.abs(rel) <= WINDOW)[None] & ((s_abs >= 0) & (s_abs < S))[:, None, :]
    slopes = 2.0 ** (-8.0 * (jnp.arange(Hq, dtype=jnp.float32) + 1.0) / Hq)
    alibi = -slopes.reshape(Hkv, G)[:, :, None, None, None] * jnp.abs(rel).astype(jnp.float32)
    s = jnp.einsum('bkgnid,bknmd->bkgnim', qb, kw, preferred_element_type=jnp.float32) * scale
    s = jnp.where(valid, s + alibi, NEG_INF)
    sink_col = jnp.broadcast_to(sink.astype(jnp.float32).reshape(1, Hkv, G, 1, 1, 1), s.shape[:-1] + (1,))
    p = jax.nn.softmax(jnp.concatenate([s, sink_col], axis=-1), axis=-1)[..., :-1]
    out = jnp.einsum('bkgnim,bknmd->bkgnid', p.astype(v.dtype), vw)
    return out.reshape(B, Hq, S, dh)


def dense_attention(q, k, v, scale):
    B, Hq, S, dq = q.shape
    Hkv = k.shape[1]
    G = Hq // Hkv
    dv = v.shape[-1]
    nb = S // Q_BLOCK
    qb = q.reshape(B, Hkv, G, nb, Q_BLOCK, dq).transpose(3, 0, 1, 2, 4, 5)

    def block(qi):
        s = jnp.einsum('bkgid,bksd->bkgis', qi, k, preferred_element_type=jnp.float32) * scale
        p = jax.nn.softmax(s, axis=-1)
        return jnp.einsum('bkgis,bksd->bkgid', p.astype(v.dtype), v)

    out = lax.map(block, qb)
    return out.transpose(1, 2, 3, 0, 4, 5).reshape(B, Hq, S, dv)


def hybrid_mixer(h, rope_tabs, w_in, na_rel_bias, win_sink, mla_q_norm_g, mla_kv_norm_g, w_uq, w_ukv,
                 ax_q_norm_g, ax_k_norm_g, w_branch, w_out):
    B, S, _ = h.shape
    cos_t, sin_t, cos_r, sin_r, cos_c, sin_c = rope_tabs
    proj = h @ w_in
    (na_q, na_k, na_v, win_q, win_k, win_v, c_q, c_kv, k_rope,
     ax_q, ax_k, ax_v, gate_logits) = jnp.split(proj, np.cumsum(IN_SIZES)[:-1].tolist(), axis=-1)
    scale = HEAD_DIM ** -0.5
    y_a = neighbourhood_attention(to_heads(na_q, NA_HEADS), to_heads(na_k, NA_HEADS),
                                  to_heads(na_v, NA_HEADS), na_rel_bias, scale)
    y_b = window_attention(to_heads(win_q, WIN_HEADS), to_heads(win_k, WIN_KV_HEADS),
                           to_heads(win_v, WIN_KV_HEADS), win_sink, scale)
    q_c = to_heads(rmsnorm(c_q, mla_q_norm_g) @ w_uq, MLA_HEADS)
    q_nope, q_pe = jnp.split(q_c, [MLA_NOPE], axis=-1)
    kv_c = to_heads(rmsnorm(c_kv, mla_kv_norm_g) @ w_ukv, MLA_HEADS)
    k_nope, v_c = jnp.split(kv_c, [MLA_NOPE], axis=-1)
    k_pe = apply_rope(k_rope[:, None], cos_t, sin_t)
    q_full = jnp.concatenate([q_nope, apply_rope(q_pe, cos_t, sin_t)], axis=-1)
    k_full = jnp.concatenate([k_nope, jnp.broadcast_to(k_pe, (B, MLA_HEADS, S, MLA_ROPE))], axis=-1)
    y_c = dense_attention(q_full, k_full, v_c, (MLA_NOPE + MLA_ROPE) ** -0.5)
    q_d = axial_rope(rmsnorm(to_heads(ax_q, AX_HEADS), ax_q_norm_g), cos_r, sin_r, cos_c, sin_c)
    k_d = axial_rope(rmsnorm(to_heads(ax_k, AX_KV_HEADS), ax_k_norm_g), cos_r, sin_r, cos_c, sin_c)
    y_d = dense_attention(q_d, k_d, to_heads(ax_v, AX_KV_HEADS), scale)
    gates = jax.nn.sigmoid(gate_logits.reshape(B, S, N_BRANCH, D_MODEL))
    branches = (y_a, y_b, y_c, y_d)
    merged = gates[:, :, 0] * (from_heads(branches[0]) @ w_branch[0])
    for n in range(1, N_BRANCH):
        merged = merged + gates[:, :, n] * (from_heads(branches[n]) @ w_branch[n])
    return merged @ w_out


def hierarchical_moe(h, w_group, b_group, w_router, b_router, w_exp1, w_exp3, w_exp2):
    B, S, D = h.shape
    t = h.reshape(-1, D)
    N = t.shape[0]
    g_logits = (t @ w_group).astype(jnp.float32) + b_group.astype(jnp.float32)
    g_prob = jax.nn.softmax(g_logits, axis=-1)
    g_sel = jnp.argmax(g_logits, axis=-1)
    g_w = jnp.take_along_axis(g_prob, g_sel[:, None], axis=-1)
    e_logits = ((t @ w_router).astype(jnp.float32) + b_router.astype(jnp.float32)).reshape(N, N_GROUPS, EXPERTS_PER_GROUP)
    e_logits = jnp.take_along_axis(e_logits, g_sel[:, None, None], axis=1)[:, 0]
    e_prob = jax.nn.softmax(e_logits, axis=-1)
    top_p, top_i = lax.top_k(e_prob, TOP_K)
    weights = g_w * top_p / jnp.sum(top_p, axis=-1, keepdims=True)
    expert_id = g_sel[:, None] * EXPERTS_PER_GROUP + top_i
    combine = jnp.sum(jax.nn.one_hot(expert_id, N_EXPERTS, dtype=jnp.float32) * weights[..., None], axis=1)

    def expert_step(acc, params):
        w1e, w3e, w2e, ce = params
        hid = jax.nn.silu(t @ w1e) * (t @ w3e)
        return acc + ce[:, None].astype(t.dtype) * (hid @ w2e), None

    out, _ = lax.scan(expert_step, jnp.zeros_like(t), (w_exp1, w_exp3, w_exp2, combine.T))
    return out.reshape(B, S, D)


def setup_inputs(seed: int = 0) -> dict:
    key = jax.random.key(seed)
    ks = jax.random.split(key, 25)
    L, D = DEPTH, D_MODEL

    def nrm(k, shape, scale):
        return jax.random.normal(k, shape, jnp.float32) * scale

    return {
        'x': nrm(ks[0], (BATCH, SEQ, D), 1.0),
        'c': nrm(ks[1], (BATCH, D), 1.0),
        'w_ada': nrm(ks[2], (L, D, N_MOD * D), 0.5 * D ** -0.5),
        'b_ada': nrm(ks[3], (L, N_MOD * D), 0.01),
        'norm1_g': 1.0 + nrm(ks[4], (L, D), 0.01),
        'norm2_g': 1.0 + nrm(ks[5], (L, D), 0.01),
        'w_in': nrm(ks[6], (L, D, D_IN), D ** -0.5),
        'na_rel_bias': nrm(ks[7], (L, NA_HEADS, 2 * NA_KH_MAX - 1, 2 * NA_KW - 1), 0.1),
        'win_sink': nrm(ks[8], (L, WIN_HEADS), 0.5),
        'mla_q_norm_g': 1.0 + nrm(ks[9], (L, MLA_Q_RANK), 0.01),
        'mla_kv_norm_g': 1.0 + nrm(ks[10], (L, MLA_KV_RANK), 0.01),
        'w_uq': nrm(ks[11], (L, MLA_Q_RANK, MLA_HEADS * (MLA_NOPE + MLA_ROPE)), MLA_Q_RANK ** -0.5),
        'w_ukv': nrm(ks[12], (L, MLA_KV_RANK, MLA_HEADS * (MLA_NOPE + MLA_V)), MLA_KV_RANK ** -0.5),
        'ax_q_norm_g': 1.0 + nrm(ks[13], (L, HEAD_DIM), 0.01),
        'ax_k_norm_g': 1.0 + nrm(ks[14], (L, HEAD_DIM), 0.01),
        'w_branch': nrm(ks[15], (L, N_BRANCH, BRANCH_W, D), BRANCH_W ** -0.5),
        'w_out': nrm(ks[16], (L, D, D), D ** -0.5),
        'w_group': nrm(ks[17], (L, D, N_GROUPS), D ** -0.5),
        'b_group': nrm(ks[18], (L, N_GROUPS), 0.01),
        'w_router': nrm(ks[19], (L, D, N_EXPERTS), D ** -0.5),
        'b_router': nrm(ks[20], (L, N_EXPERTS), 0.01),
        'w_exp1': nrm(ks[21], (L, N_EXPERTS, D, D_EXPERT), D ** -0.5),
        'w_exp3': nrm(ks[22], (L, N_EXPERTS, D, D_EXPERT), D ** -0.5),
        'w_exp2': nrm(ks[23], (L, N_EXPERTS, D_EXPERT, D), D_EXPERT ** -0.5),
        'final_norm_g': 1.0 + nrm(ks[24], (D,), 0.01),
    }


def reference(x, c, w_ada, b_ada, norm1_g, norm2_g, w_in, na_rel_bias, win_sink, mla_q_norm_g,
              mla_kv_norm_g, w_uq, w_ukv, ax_q_norm_g, ax_k_norm_g, w_branch, w_out, w_group, b_group,
              w_router, b_router, w_exp1, w_exp3, w_exp2, final_norm_g):
    S = x.shape[1]
    pos = jnp.arange(S)
    cos_t, sin_t = rope_angles(pos, MLA_ROPE)
    cos_r, sin_r = rope_angles(pos // GRID_W, HEAD_DIM // 2)
    cos_c, sin_c = rope_angles(pos % GRID_W, HEAD_DIM // 2)
    rope_tabs = (cos_t, sin_t, cos_r, sin_r, cos_c, sin_c)
    c_act = jax.nn.silu(c)
    for l in range(DEPTH):
        mod = c_act @ w_ada[l] + b_ada[l]
        sh1, sc1, g1, sh2, sc2, g2 = jnp.split(mod[:, None, :], N_MOD, axis=-1)
        h = rmsnorm(x, norm1_g[l]) * (1.0 + sc1) + sh1
        x = x + g1 * hybrid_mixer(h, rope_tabs, w_in[l], na_rel_bias[l], win_sink[l], mla_q_norm_g[l],
                                  mla_kv_norm_g[l], w_uq[l], w_ukv[l], ax_q_norm_g[l], ax_k_norm_g[l],
                                  w_branch[l], w_out[l])
        h = rmsnorm(x, norm2_g[l]) * (1.0 + sc2) + sh2
        x = x + g2 * hierarchical_moe(h, w_group[l], b_group[l], w_router[l], b_router[l],
                                      w_exp1[l], w_exp3[l], w_exp2[l])
    return rmsnorm(x, final_norm_g)
```

```python
import functools

import jax
import jax.numpy as jnp
import numpy as np
from jax import lax
from jax.experimental import pallas as pl
from jax.experimental.pallas import tpu as pltpu

D_MODEL = 1024
GRID_W = 64
HEAD_DIM = 64
NA_HEADS = 4
NA_KH = 8
NA_KW = 16
WIN_HEADS = 4
WIN_KV_HEADS = 2
WINDOW = 128
MLA_HEADS = 4
MLA_Q_RANK = 192
MLA_KV_RANK = 256
MLA_NOPE = 64
MLA_ROPE = 32
MLA_V = 64
AX_HEADS = 4
AX_KV_HEADS = 2
ROPE_THETA = 10000.0
N_BRANCH = 4
BRANCH_W = 256
N_GROUPS = 4
EXPERTS_PER_GROUP = 8
N_EXPERTS = N_GROUPS * EXPERTS_PER_GROUP
D_EXPERT = 256
N_MOD = 6
EPS = 1e-6
NEG_INF = -1e30

LANES = 128

_IN_SIZES = (256, 256, 256, 256, 128, 128, MLA_Q_RANK, MLA_KV_RANK, MLA_ROPE, 256, 128, 128, N_BRANCH * D_MODEL)
_IN_OFF = np.concatenate([[0], np.cumsum(_IN_SIZES)])
(_O_NAQ, _O_NAK, _O_NAV, _O_WQ, _O_WK, _O_WV, _O_CQ, _O_CKV, _O_KR, _O_AQ, _O_AK, _O_AV, _O_GATE,
 _D_IN) = [int(v) for v in _IN_OFF]

VMEM_LIMIT = 56 * 1024 * 1024

BF16 = jnp.bfloat16
F32 = jnp.float32


def _cparams(sem):
    return pltpu.CompilerParams(dimension_semantics=sem, vmem_limit_bytes=VMEM_LIMIT)


_QPERM = np.concatenate([np.arange(0, 64), np.arange(128, 192), np.arange(64, 128), np.arange(192, 256)])


def _swap_halves(n_heads, width, half):
    idx = []
    for h in range(n_heads):
        for c in range(0, width, 2 * half):
            idx += list(range(h * width + c + half, h * width + c + 2 * half))
            idx += list(range(h * width + c, h * width + c + half))
    return np.asarray(idx)


def _main_cols():
    z = _O_GATE
    cols = []
    cols += list(range(_O_NAQ, _O_NAV + 256))
    cols += list(_O_WQ + _QPERM) + list(range(_O_WK, _O_WV + 128))
    cols += list(range(_O_CQ, _O_CQ + MLA_Q_RANK)) + [z] * 64
    cols += list(range(_O_CKV, _O_CKV + MLA_KV_RANK))
    kr = np.arange(_O_KR, _O_KR + MLA_ROPE)
    cols += [z] * 64 + list(kr) + [z] * 32
    cols += [z] * 64 + list(kr[_swap_halves(1, 32, 16)]) + [z] * 32
    aq = _O_AQ + _QPERM
    cols += list(aq) + list(aq[_swap_halves(4, 64, 16)])
    ak = np.arange(_O_AK, _O_AK + 128)
    cols += list(ak) + list(ak[_swap_halves(2, 64, 16)])
    cols += list(range(_O_AV, _O_AV + 128))
    return np.asarray(cols, dtype=np.int32)


_MAIN_COLS = _main_cols()
_N_MAIN = int(_MAIN_COLS.shape[0])
_G_NA, _G_WIN, _G_CQ, _G_CKV, _G_KPE, _G_AXQ, _G_AXK, _G_AXV = 0, 768, 1280, 1536, 1792, 2048, 2560, 2816


def _uq_cols():
    z = MLA_HEADS * (MLA_NOPE + MLA_ROPE)
    a, b = [], []
    for h in range(MLA_HEADS):
        base = h * (MLA_NOPE + MLA_ROPE)
        rope = np.arange(base + MLA_NOPE, base + MLA_NOPE + MLA_ROPE)
        a += list(range(base, base + MLA_NOPE)) + list(rope) + [z] * 32
        b += [z] * 64 + list(rope[_swap_halves(1, 32, 16)]) + [z] * 32
    return np.asarray(a + b, dtype=np.int32)


def _ukv_cols():
    z = MLA_HEADS * (MLA_NOPE + MLA_V)
    k, v = [], []
    for h in range(MLA_HEADS):
        base = h * (MLA_NOPE + MLA_V)
        k += list(range(base, base + MLA_NOPE)) + [z] * 64
        v += list(range(base + MLA_NOPE, base + MLA_NOPE + MLA_V))
    return np.asarray(k + v, dtype=np.int32)


_UQ_COLS = _uq_cols()
_UKV_COLS = _ukv_cols()


def _take_cols(w, cols):
    wz = jnp.concatenate([w, jnp.zeros(w.shape[:-1] + (1,), w.dtype)], axis=-1)
    return jnp.take(wz, jnp.asarray(cols), axis=-1)


def _rope_tables(S):
    pos = jnp.arange(S)

    def angles(p, dim):
        inv = ROPE_THETA ** (-jnp.arange(0, dim, 2, dtype=F32) / dim)
        ang = p.astype(F32)[:, None] * inv[None, :]
        return jnp.cos(ang), jnp.sin(ang)

    cos_t, sin_t = angles(pos, MLA_ROPE)
    cos_r, sin_r = angles(pos // GRID_W, HEAD_DIM // 2)
    cos_c, sin_c = angles(pos % GRID_W, HEAD_DIM // 2)
    ones = jnp.ones((S, 64), F32)
    zeros32 = jnp.zeros((S, 32), F32)
    zeros64 = jnp.zeros((S, 64), F32)
    mla_cos = jnp.concatenate([ones, cos_t, cos_t, zeros32], axis=1)
    mla_sin = jnp.concatenate([zeros64, -sin_t, sin_t, zeros32], axis=1)
    ax_cos64 = jnp.concatenate([cos_r, cos_r, cos_c, cos_c], axis=1)
    ax_sin64 = jnp.concatenate([-sin_r, sin_r, -sin_c, sin_c], axis=1)
    ax_cos = jnp.concatenate([ax_cos64, ax_cos64], axis=1)
    ax_sin = jnp.concatenate([ax_sin64, ax_sin64], axis=1)
    return mla_cos, mla_sin, ax_cos, ax_sin


def _na_bias_tables(rel_bias):
    col = np.arange(GRID_W)
    col_start = np.clip(col - NA_KW // 2, 0, GRID_W - NA_KW)
    col_off = col[None, :] - col_start[:, None]
    col_in = (col_off >= 0) & (col_off < NA_KW)
    dc_idx = np.clip(col[None, :] - col[:, None], -(NA_KW - 1), NA_KW - 1) + NA_KW - 1
    rb = rel_bias.astype(F32)[..., jnp.asarray(dc_idx)]
    rb = jnp.where(jnp.asarray(col_in)[None, None, None], rb, NEG_INF)
    tabs = []
    for v in range(NA_KH):
        t = rb[:, :, v:v + NA_KH]
        t = jnp.transpose(t, (0, 1, 3, 2, 4)).reshape(rb.shape[0], rb.shape[1], GRID_W, NA_KH * GRID_W)
        tabs.append(t)
    return jnp.stack(tabs, axis=2)


def _ada_kernel(c_ref, w_ref, b_ref, o_ref):
    c = c_ref[...]
    ca = c * jax.nn.sigmoid(c)
    o_ref[0] = jnp.dot(ca, w_ref[0], preferred_element_type=F32, precision=lax.Precision.HIGHEST) + b_ref[0]


def _ada_call(c, w_ada, b_ada):
    L, D, M = w_ada.shape
    B = c.shape[0]
    tn = 1536
    return pl.pallas_call(
        _ada_kernel,
        out_shape=jax.ShapeDtypeStruct((L, B, M), F32),
        grid=(L, M // tn),
        in_specs=[
            pl.BlockSpec((B, D), lambda l, j: (0, 0)),
            pl.BlockSpec((1, D, tn), lambda l, j: (l, 0, j)),
            pl.BlockSpec((1, 1, tn), lambda l, j: (l, 0, j)),
        ],
        out_specs=pl.BlockSpec((1, B, tn), lambda l, j: (l, 0, j)),
        compiler_params=_cparams(("parallel", "parallel")),
        name="ada_mod",
    )(c, w_ada, b_ada.reshape(L, 1, M))


def _modulated_norm(x, g, shift, scale):
    ms = jnp.mean(x * x, axis=-1, keepdims=True)
    return (x * lax.rsqrt(ms + EPS) * g) * (1.0 + scale) + shift


def _head_sumsq(x, gmat):
    sq = x * x
    hi = sq.astype(BF16)
    lo = (sq - hi.astype(F32)).astype(BF16)
    return (jnp.dot(hi, gmat, preferred_element_type=F32) + jnp.dot(lo, gmat, preferred_element_type=F32))


def _tile_lanes(t, reps):
    return t if reps == 1 else jnp.concatenate([t] * reps, axis=1)


def _proj_kernel(x_ref, mod_ref, g1_ref, wm_ref, wuq_ref, wukv_ref, qg_ref, kvg_ref, axg_ref, gmat_ref,
                 mcos_ref, msin_ref, acos_ref, asin_ref,
                 na_ref, win_ref, mla_ref, ax_ref):
    x = x_ref[...]
    h = _modulated_norm(x, g1_ref[...], mod_ref[0, 0:1, :], mod_ref[0, 1:2, :]).astype(BF16)

    def proj(lo, width):
        return jnp.dot(h, wm_ref[:, lo:lo + width], preferred_element_type=F32)

    att_scale = HEAD_DIM ** -0.5
    na = proj(_G_NA, 768)
    na_ref[:, 0:256] = (na[:, 0:256] * att_scale).astype(BF16)
    na_ref[:, 256:768] = na[:, 256:768].astype(BF16)
    wn = proj(_G_WIN, 512)
    win_ref[:, 0:256] = (wn[:, 0:256] * att_scale).astype(BF16)
    win_ref[:, 256:512] = wn[:, 256:512].astype(BF16)

    mcos = mcos_ref[...]
    msin = msin_ref[...]
    cq = proj(_G_CQ, 256)
    cqn = (cq * lax.rsqrt(jnp.sum(cq * cq, axis=-1, keepdims=True) * (1.0 / MLA_Q_RANK) + EPS)
           * qg_ref[...]).astype(BF16)
    qq = jnp.dot(cqn, wuq_ref[...], preferred_element_type=F32)
    mcos4 = _tile_lanes(mcos, 4)
    msin4 = _tile_lanes(msin, 4)
    mla_scale = (MLA_NOPE + MLA_ROPE) ** -0.5
    mla_ref[:, 0:512] = ((qq[:, 0:512] * mcos4 + qq[:, 512:1024] * msin4) * mla_scale).astype(BF16)
    ckv = proj(_G_CKV, 256)
    ckvn = (ckv * lax.rsqrt(jnp.mean(ckv * ckv, axis=-1, keepdims=True) + EPS) * kvg_ref[...]).astype(BF16)
    kv = jnp.dot(ckvn, wukv_ref[...], preferred_element_type=F32)
    kp = proj(_G_KPE, 256)
    kpe = kp[:, 0:128] * mcos + kp[:, 128:256] * msin
    mla_ref[:, 512:1024] = (kv[:, 0:512] + _tile_lanes(kpe, 4)).astype(BF16)
    mla_ref[:, 1024:1280] = kv[:, 512:768].astype(BF16)

    acos = acos_ref[...]
    asin = asin_ref[...]
    gmat = gmat_ref[...]
    aq = proj(_G_AXQ, 512)
    q0 = aq[:, 0:256]
    rq = lax.rsqrt(_head_sumsq(q0, gmat) * (1.0 / HEAD_DIM) + EPS)
    qd = (q0 * (axg_ref[0:1, :] * _tile_lanes(acos, 2)) + aq[:, 256:512] * (axg_ref[1:2, :] * _tile_lanes(asin, 2)))
    ax_ref[:, 0:256] = (qd * (rq * att_scale)).astype(BF16)
    ak = proj(_G_AXK, 256)
    k0 = ak[:, 0:128]
    rk = lax.rsqrt(_head_sumsq(k0, gmat[0:128, 0:128]) * (1.0 / HEAD_DIM) + EPS)
    kd = k0 * (axg_ref[2:3, 0:128] * acos) + ak[:, 128:256] * (axg_ref[3:4, 0:128] * asin)
    ax_ref[:, 256:384] = (kd * rk).astype(BF16)
    ax_ref[:, 384:512] = proj(_G_AXV, 128).astype(BF16)


def _proj_call(x2, mod_l, g1, wm, wuq, wukv, qg, kvg, axg, gmat, tabs, B, S):
    N, D = x2.shape
    tm = min(512, S)
    tps = S // tm
    mcos, msin, acos, asin = tabs
    row = lambda i: (i, 0)
    const = lambda i: (0, 0)
    pos = lambda i: (i % tps, 0)
    return pl.pallas_call(
        _proj_kernel,
        out_shape=(jax.ShapeDtypeStruct((N, 768), BF16), jax.ShapeDtypeStruct((N, 512), BF16),
                   jax.ShapeDtypeStruct((N, 1280), BF16), jax.ShapeDtypeStruct((N, 512), BF16)),
        grid=(N // tm,),
        in_specs=[
            pl.BlockSpec((tm, D), row),
            pl.BlockSpec((1, N_MOD, D), lambda i: (i // tps, 0, 0)),
            pl.BlockSpec((1, D), const),
            pl.BlockSpec((D, _N_MAIN), const),
            pl.BlockSpec(wuq.shape, const),
            pl.BlockSpec(wukv.shape, const),
            pl.BlockSpec((1, 256), const),
            pl.BlockSpec((1, 256), const),
            pl.BlockSpec((4, 256), const),
            pl.BlockSpec((256, 256), const),
            pl.BlockSpec((tm, LANES), pos), pl.BlockSpec((tm, LANES), pos),
            pl.BlockSpec((tm, LANES), pos), pl.BlockSpec((tm, LANES), pos),
        ],
        out_specs=(pl.BlockSpec((tm, 768), row), pl.BlockSpec((tm, 512), row),
                   pl.BlockSpec((tm, 1280), row), pl.BlockSpec((tm, 512), row)),
        compiler_params=_cparams(("parallel",)),
        name="proj",
    )(x2, mod_l, g1, wm, wuq, wukv, qg, kvg, axg, gmat, mcos, msin, acos, asin)


def _half_mask(shape, half):
    lane = lax.broadcasted_iota(jnp.int32, shape, len(shape) - 1)
    return (lane < 64) if half == 0 else (lane >= 64)


def _qk(q, k):
    return lax.dot_general(q, k, (((1,), (1,)), ((), ())), preferred_element_type=F32)


def _na_kernel(q_ref, k_ref, v_ref, bias_ref, o_ref, *, rows, rows_per_step):
    j = pl.program_id(1)

    def body(r, carry):
        R = j * rows_per_step + r
        row_start = jnp.clip(R - NA_KH // 2, 0, rows - NA_KH)
        variant = row_start - R + (NA_KH - 1)
        q_off = pl.multiple_of(r * GRID_W, GRID_W)
        k_off = pl.multiple_of(row_start * GRID_W, GRID_W)
        q = q_ref[pl.ds(q_off, GRID_W), :]
        ks = k_ref[pl.ds(k_off, NA_KH * GRID_W), :]
        vs = v_ref[pl.ds(k_off, NA_KH * GRID_W), :]
        outs = []
        for pair in range(NA_HEADS // 2):
            qp = q[:, pair * 128:(pair + 1) * 128]
            kp = ks[:, pair * 128:(pair + 1) * 128]
            vp = vs[:, pair * 128:(pair + 1) * 128]
            halves = []
            for half in range(2):
                qh = jnp.where(_half_mask(qp.shape, half), qp, jnp.zeros_like(qp))
                s = _qk(qh, kp) + bias_ref[pair * 2 + half, variant]
                m = jnp.max(s, axis=-1, keepdims=True)
                p = jnp.exp(s - m)
                l = jnp.sum(p, axis=-1, keepdims=True)
                halves.append(jnp.dot(p.astype(BF16), vp, preferred_element_type=F32) / l)
            outs.append(jnp.where(_half_mask(halves[0].shape, 0), halves[0], halves[1]))
        o_ref[pl.ds(q_off, GRID_W), :] = jnp.concatenate(outs, axis=1).astype(o_ref.dtype)
        return carry

    lax.fori_loop(0, rows_per_step, body, 0)


def _na_call(na, bias_tab, B, S):
    rows = S // GRID_W
    rps = min(8, rows)
    tq = rps * GRID_W
    spb = S // tq
    return pl.pallas_call(
        functools.partial(_na_kernel, rows=rows, rows_per_step=rps),
        out_shape=jax.ShapeDtypeStruct((B * S, 256), BF16),
        grid=(B, spb),
        in_specs=[
            pl.BlockSpec((tq, 256), lambda b, j: (b * spb + j, 0)),
            pl.BlockSpec((S, 256), lambda b, j: (b, 1)),
            pl.BlockSpec((S, 256), lambda b, j: (b, 2)),
            pl.BlockSpec(bias_tab.shape, lambda b, j: (0, 0, 0, 0)),
        ],
        out_specs=pl.BlockSpec((tq, 256), lambda b, j: (b * spb + j, 0)),
        compiler_params=_cparams(("parallel", "arbitrary")),
        name="attn_na",
    )(na, na, na, bias_tab)


_WIN_SLOPES = tuple(2.0 ** (-8.0 * (h + 1.0) / WIN_HEADS) for h in range(WIN_HEADS))
_WIN_BAND = 3 * WINDOW


def _win_kernel(sink_ref, q_ref, k_ref, v_ref, o_ref, *, S):
    j = pl.program_id(1)
    start = jnp.clip((j - 1) * WINDOW, 0, S - _WIN_BAND)
    k_off = pl.multiple_of(start, WINDOW)
    kb = k_ref[pl.ds(k_off, _WIN_BAND), :]
    vb = v_ref[pl.ds(k_off, _WIN_BAND), :]
    q = q_ref[...]
    shape = (2 * WINDOW, _WIN_BAND)
    row = lax.broadcasted_iota(jnp.int32, shape, 0)
    colm = lax.broadcasted_iota(jnp.int32, shape, 1)
    rel = (start - j * WINDOW) + colm - (row % WINDOW)
    dist = jnp.abs(rel)
    valid = dist <= WINDOW
    distf = dist.astype(F32)
    first = row < WINDOW
    outs = []
    for kvh in range(WIN_KV_HEADS):
        qs = jnp.concatenate([q[:, 0:128], q[:, 128:256]], axis=0)
        qs = jnp.where(_half_mask(qs.shape, kvh), qs, jnp.zeros_like(qs))
        h0, h1 = 2 * kvh, 2 * kvh + 1
        slope = jnp.where(first, _WIN_SLOPES[h0], _WIN_SLOPES[h1])
        s = jnp.where(valid, _qk(qs, kb) - slope * distf, NEG_INF)
        rowc = lax.broadcasted_iota(jnp.int32, (2 * WINDOW, 1), 0)
        sink = jnp.where(rowc < WINDOW, sink_ref[h0], sink_ref[h1])
        m = jnp.maximum(jnp.max(s, axis=-1, keepdims=True), sink)
        p = jnp.exp(s - m)
        l = jnp.sum(p, axis=-1, keepdims=True) + jnp.exp(sink - m)
        outs.append(jnp.dot(p.astype(BF16), vb, preferred_element_type=F32) / l)
    sel = _half_mask(outs[0].shape, 0)
    o = jnp.where(sel, outs[0], outs[1])
    o_ref[:, 0:128] = o[0:WINDOW].astype(o_ref.dtype)
    o_ref[:, 128:256] = o[WINDOW:2 * WINDOW].astype(o_ref.dtype)


def _win_call(win, sink, B, S):
    nb = S // WINDOW
    return pl.pallas_call(
        functools.partial(_win_kernel, S=S),
        out_shape=jax.ShapeDtypeStruct((B * S, 256), BF16),
        grid=(B, nb),
        in_specs=[
            pl.BlockSpec(memory_space=pltpu.SMEM),
            pl.BlockSpec((WINDOW, 256), lambda b, j: (b * nb + j, 0)),
            pl.BlockSpec((S, 128), lambda b, j: (b, 2)),
            pl.BlockSpec((S, 128), lambda b, j: (b, 3)),
        ],
        out_specs=pl.BlockSpec((WINDOW, 256), lambda b, j: (b * nb + j, 0)),
        compiler_params=_cparams(("parallel", "arbitrary")),
        name="attn_win",
    )(sink, win, win, win)


def _dense_kernel(q_ref, k_ref, v_ref, o_ref, *, heads):
    outs = {}
    for (qb, qhalf, kb, vb, ob, ohalf) in heads:
        q = q_ref[:, qb * 128:(qb + 1) * 128]
        if qhalf is not None:
            q = jnp.where(_half_mask(q.shape, qhalf), q, jnp.zeros_like(q))
        s = _qk(q, k_ref[:, kb * 128:(kb + 1) * 128])
        m = jnp.max(s, axis=-1, keepdims=True)
        p = jnp.exp(s - m)
        l = jnp.sum(p, axis=-1, keepdims=True)
        o = jnp.dot(p.astype(BF16), v_ref[:, vb * 128:(vb + 1) * 128], preferred_element_type=F32) / l
        outs[(ob, ohalf)] = o
    n_out = o_ref.shape[1] // 128
    for ob in range(n_out):
        o = jnp.where(_half_mask(outs[(ob, 0)].shape, 0), outs[(ob, 0)], outs[(ob, 1)])
        o_ref[:, ob * 128:(ob + 1) * 128] = o.astype(o_ref.dtype)


_MLA_HEAD_SPEC = tuple((h, None, h, h // 2, h // 2, h % 2) for h in range(MLA_HEADS))
_AX_HEAD_SPEC = ((0, 0, 0, 0, 0, 0), (0, 1, 0, 0, 0, 1), (1, 0, 0, 0, 1, 0), (1, 1, 0, 0, 1, 1))


def _dense_call(arr, B, S, *, q_w, k_w, v_w, heads, name):
    tq = min(256, S)
    nq = S // tq
    assert q_w % k_w == 0 and (q_w + k_w) % v_w == 0
    return pl.pallas_call(
        functools.partial(_dense_kernel, heads=heads),
        out_shape=jax.ShapeDtypeStruct((B * S, 256), BF16),
        grid=(B, nq),
        in_specs=[
            pl.BlockSpec((tq, q_w), lambda b, j: (b * nq + j, 0)),
            pl.BlockSpec((S, k_w), lambda b, j: (b, q_w // k_w)),
            pl.BlockSpec((S, v_w), lambda b, j: (b, (q_w + k_w) // v_w)),
        ],
        out_specs=pl.BlockSpec((tq, 256), lambda b, j: (b * nq + j, 0)),
        compiler_params=_cparams(("parallel", "arbitrary")),
        name=name,
    )(arr, arr, arr)


def _route(gl, el):
    lane = lax.broadcasted_iota(jnp.int32, gl.shape, 1)
    big = jnp.int32(1 << 20)
    g_ok = lane < N_GROUPS
    glm = jnp.where(g_ok, gl, -jnp.inf)
    g_max = jnp.max(glm, axis=-1, keepdims=True)
    g_sel = jnp.min(jnp.where(g_ok & (glm == g_max), lane, big), axis=-1, keepdims=True)
    g_w = 1.0 / jnp.sum(jnp.where(g_ok, jnp.exp(glm - g_max), 0.0), axis=-1, keepdims=True)
    e_ok = (lane >= g_sel * EXPERTS_PER_GROUP) & (lane < (g_sel + 1) * EXPERTS_PER_GROUP)
    elm = jnp.where(e_ok, el, -jnp.inf)
    e_max = jnp.max(elm, axis=-1, keepdims=True)
    e_exp = jnp.where(e_ok, jnp.exp(elm - e_max), 0.0)
    e_prob = e_exp / jnp.sum(e_exp, axis=-1, keepdims=True)
    i1 = jnp.min(jnp.where(e_ok & (elm == e_max), lane, big), axis=-1, keepdims=True)
    rest = e_ok & (lane != i1)
    p2 = jnp.max(jnp.where(rest, e_prob, -1.0), axis=-1, keepdims=True)
    i2 = jnp.min(jnp.where(rest & (e_prob == p2), lane, big), axis=-1, keepdims=True)
    p1 = jnp.max(jnp.where(lane == i1, e_prob, -1.0), axis=-1, keepdims=True)
    tot = p1 + p2
    return jnp.where(lane == i1, g_w * p1 / tot, 0.0) + jnp.where(lane == i2, g_w * p2 / tot, 0.0)


def _merge_kernel(x_ref, mod_ref, g1_ref, g2_ref, ya_ref, yb_ref, yc_ref, yd_ref, wg_ref, wb_ref, wo_ref,
                  wr_ref, br_ref, x1_ref, h2_ref, comb_ref):
    x = x_ref[...]
    h = _modulated_norm(x, g1_ref[...], mod_ref[0, 0:1, :], mod_ref[0, 1:2, :]).astype(BF16)
    merged = None
    for n, y_ref in enumerate((ya_ref, yb_ref, yc_ref, yd_ref)):
        gate = jax.nn.sigmoid(jnp.dot(h, wg_ref[:, n * D_MODEL:(n + 1) * D_MODEL], preferred_element_type=F32))
        term = gate * jnp.dot(y_ref[...], wb_ref[n], preferred_element_type=F32)
        merged = term if merged is None else merged + term
    mix = jnp.dot(merged.astype(BF16), wo_ref[...], preferred_element_type=F32)
    x1 = x + mod_ref[0, 2:3, :] * mix
    x1_ref[...] = x1
    h2 = _modulated_norm(x1, g2_ref[...], mod_ref[0, 3:4, :], mod_ref[0, 4:5, :])
    h2_ref[...] = h2.astype(BF16)
    logits = jnp.dot(h2, wr_ref[...], preferred_element_type=F32, precision=lax.Precision.HIGHEST) + br_ref[...]
    comb_ref[...] = _route(logits[:, 0:128], logits[:, 128:256])


def _merge_call(x2, mod_l, g1, g2, ys, wg, wb, wo, wr, br, B, S):
    N, D = x2.shape
    tm = min(256, S)
    tps = S // tm
    row = lambda i: (i, 0)
    const = lambda i: (0, 0)
    return pl.pallas_call(
        _merge_kernel,
        out_shape=(jax.ShapeDtypeStruct((N, D), F32), jax.ShapeDtypeStruct((N, D), BF16),
                   jax.ShapeDtypeStruct((N, LANES), F32)),
        grid=(N // tm,),
        in_specs=[
            pl.BlockSpec((tm, D), row),
            pl.BlockSpec((1, N_MOD, D), lambda i: (i // tps, 0, 0)),
            pl.BlockSpec((1, D), const), pl.BlockSpec((1, D), const),
            pl.BlockSpec((tm, 256), row), pl.BlockSpec((tm, 256), row),
            pl.BlockSpec((tm, 256), row), pl.BlockSpec((tm, 256), row),
            pl.BlockSpec((D, N_BRANCH * D), const),
            pl.BlockSpec((N_BRANCH, BRANCH_W, D), lambda i: (0, 0, 0)),
            pl.BlockSpec((D, D), const),
            pl.BlockSpec((D, 256), const),
            pl.BlockSpec((1, 256), const),
        ],
        out_specs=(pl.BlockSpec((tm, D), row), pl.BlockSpec((tm, D), row), pl.BlockSpec((tm, LANES), row)),
        compiler_params=_cparams(("parallel",)),
        name="merge",
    )(x2, mod_l, g1, g2, *ys, wg, wb, wo, wr, br)


def _moe_kernel(h_ref, comb_ref, x1_ref, mod_ref, w13_ref, w2_ref, o_ref, acc_ref):
    e = pl.program_id(1)

    @pl.when(e == 0)
    def _():
        acc_ref[...] = jnp.zeros_like(acc_ref)

    comb = comb_ref[...]
    lane = lax.broadcasted_iota(jnp.int32, comb.shape, 1)
    ce = jnp.sum(jnp.where(lane == e, comb, 0.0), axis=-1, keepdims=True)
    ab = jnp.dot(h_ref[...], w13_ref[0], preferred_element_type=F32)
    a = ab[:, 0:D_EXPERT]
    hid = (a * jax.nn.sigmoid(a)) * ab[:, D_EXPERT:2 * D_EXPERT] * ce
    acc_ref[...] += jnp.dot(hid.astype(BF16), w2_ref[0], preferred_element_type=F32)

    @pl.when(e == pl.num_programs(1) - 1)
    def _():
        o_ref[...] = x1_ref[...] + mod_ref[0, 5:6, :] * acc_ref[...]


def _moe_call(h2, comb, x1, mod_l, w13, w2, B, S):
    N, D = x1.shape
    tm = min(1024, S)
    tps = S // tm
    row = lambda i, e: (i, 0)
    return pl.pallas_call(
        _moe_kernel,
        out_shape=jax.ShapeDtypeStruct((N, D), F32),
        grid=(N // tm, N_EXPERTS),
        in_specs=[
            pl.BlockSpec((tm, D), row),
            pl.BlockSpec((tm, LANES), row),
            pl.BlockSpec((tm, D), row),
            pl.BlockSpec((1, N_MOD, D), lambda i, e: (i // tps, 0, 0)),
            pl.BlockSpec((1, D, 2 * D_EXPERT), lambda i, e: (e, 0, 0)),
            pl.BlockSpec((1, D_EXPERT, D), lambda i, e: (e, 0, 0)),
        ],
        out_specs=pl.BlockSpec((tm, D), row),
        scratch_shapes=[pltpu.VMEM((tm, D), F32)],
        compiler_params=_cparams(("parallel", "arbitrary")),
        name="moe",
    )(h2, comb, x1, mod_l, w13, w2)


def _final_norm_kernel(x_ref, g_ref, o_ref):
    x = x_ref[...]
    ms = jnp.mean(x * x, axis=-1, keepdims=True)
    o_ref[...] = x * lax.rsqrt(ms + EPS) * g_ref[...]


def _final_norm_call(x2, g):
    N, D = x2.shape
    tm = min(1024, N)
    return pl.pallas_call(
        _final_norm_kernel,
        out_shape=jax.ShapeDtypeStruct((N, D), F32),
        grid=(N // tm,),
        in_specs=[pl.BlockSpec((tm, D), lambda i: (i, 0)), pl.BlockSpec((1, D), lambda i: (0, 0))],
        out_specs=pl.BlockSpec((tm, D), lambda i: (i, 0)),
        compiler_params=_cparams(("parallel",)),
        name="final_norm",
    )(x2, g)


def kernel(x, c, w_ada, b_ada, norm1_g, norm2_g, w_in, na_rel_bias, win_sink, mla_q_norm_g, mla_kv_norm_g,
           w_uq, w_ukv, ax_q_norm_g, ax_k_norm_g, w_branch, w_out, w_group, b_group, w_router, b_router,
           w_exp1, w_exp3, w_exp2, final_norm_g):
    B, S, D = x.shape
    L = w_in.shape[0]
    N = B * S

    wm = _take_cols(w_in[:, :, :_O_GATE], _MAIN_COLS).astype(BF16)
    wg = w_in[:, :, _O_GATE:].astype(BF16)
    wuq = jnp.pad(_take_cols(w_uq, _UQ_COLS), ((0, 0), (0, 256 - MLA_Q_RANK), (0, 0))).astype(BF16)
    wukv = _take_cols(w_ukv, _UKV_COLS).astype(BF16)
    qg = jnp.pad(mla_q_norm_g, ((0, 0), (0, 256 - MLA_Q_RANK))).reshape(L, 1, 256)
    kvg = mla_kv_norm_g.reshape(L, 1, 256)
    sw64 = _swap_halves(1, 64, 16)
    axg = jnp.stack([jnp.tile(ax_q_norm_g, (1, 4)), jnp.tile(ax_q_norm_g[:, sw64], (1, 4)),
                     jnp.tile(ax_k_norm_g, (1, 4)), jnp.tile(ax_k_norm_g[:, sw64], (1, 4))], axis=1)
    gmat = jnp.asarray(np.kron(np.eye(4), np.ones((64, 64))), BF16)
    tabs = _rope_tables(S)
    bias_tab = _na_bias_tables(na_rel_bias)
    wb = w_branch.astype(BF16)
    perm_rows = jnp.asarray(_QPERM)
    wb = wb.at[:, 1].set(wb[:, 1][:, perm_rows]).at[:, 3].set(wb[:, 3][:, perm_rows])
    wo = w_out.astype(BF16)
    wr = jnp.concatenate([jnp.pad(w_group, ((0, 0), (0, 0), (0, LANES - N_GROUPS))),
                          jnp.pad(w_router, ((0, 0), (0, 0), (0, LANES - N_EXPERTS)))], axis=-1)
    br = jnp.concatenate([jnp.pad(b_group, ((0, 0), (0, LANES - N_GROUPS))),
                          jnp.pad(b_router, ((0, 0), (0, LANES - N_EXPERTS)))], axis=-1).reshape(L, 1, 256)
    w13 = jnp.concatenate([w_exp1, w_exp3], axis=-1).astype(BF16)
    w2 = w_exp2.astype(BF16)

    mod = _ada_call(c, w_ada, b_ada).reshape(L, B, N_MOD, D)
    xf = x.reshape(N, D)
    for l in range(L):
        mod_l = mod[l]
        g1 = norm1_g[l].reshape(1, D)
        g2 = norm2_g[l].reshape(1, D)
        na, win, mla, ax = _proj_call(xf, mod_l, g1, wm[l], wuq[l], wukv[l], qg[l], kvg[l], axg[l], gmat,
                                      tabs, B, S)
        y_a = _na_call(na, bias_tab[l], B, S)
        y_b = _win_call(win, win_sink[l], B, S)
        y_c = _dense_call(mla, B, S, q_w=512, k_w=512, v_w=256, heads=_MLA_HEAD_SPEC, name="attn_mla")
        y_d = _dense_call(ax, B, S, q_w=256, k_w=128, v_w=128, heads=_AX_HEAD_SPEC, name="attn_ax")
        x1, h2, comb = _merge_call(xf, mod_l, g1, g2, (y_a, y_b, y_c, y_d), wg[l], wb[l], wo[l], wr[l], br[l],
                                   B, S)
        xf = _moe_call(h2, comb, x1, mod_l, w13[l], w2[l], B, S)
    return _final_norm_call(xf, final_norm_g.reshape(1, D)).reshape(B, S, D)
```
